```python
import jax
import jax.numpy as jnp
from jax import lax
import numpy as np

D_MODEL = 1024
BATCH = 16
SEQ = 256
DEPTH = 4
DEC_BATCH = 2
DEC_SEQ = 1024
PAST_LEN = 256

GRID_W = 64
EPS = 1e-6
QBLOCK = 128
ROPE_BASE = 10000.0
H_A = 8
NOPE_DIM = 64
ROPE_DIM = 32
V_DIM = 64
QK_DIM = NOPE_DIM + ROPE_DIM
Q_RANK = 256
KV_RANK = 128
W_A = H_A * V_DIM
H_B = 4
DH_B = 128
W_B = H_B * DH_B
CHUNK = 64
W_C = 512
CONV_W = 3
H_D = 8
DH_D = 64
W_D = H_D * DH_D
WIN_R = 8
WIN_C = 16
EVEN_SPLITS = (Q_RANK, KV_RANK, ROPE_DIM, W_A, W_B, W_B, W_B, 2 * H_B, 2 * H_B, W_B, W_B)
EVEN_IN = Q_RANK + KV_RANK + ROPE_DIM + W_A + 5 * W_B + 4 * H_B
ODD_SPLITS = (W_C, W_C, W_C, W_C, W_D, W_D, W_D, W_D)
ODD_IN = 4 * W_C + 4 * W_D

kernel_name = 'hybrid_mla_mlstm_conv_natten_diffusion_step'


def rms_norm(x, g):
    xf = x.astype(jnp.float32)
    y = xf * lax.rsqrt(jnp.mean(xf * xf, axis=-1, keepdims=True) + EPS)
    return (y * g.astype(jnp.float32)).astype(x.dtype)


def split_cols(u, sizes):
    out, start = [], 0
    for sz in sizes:
        out.append(u[..., start:start + sz])
        start += sz
    return out


def modulation(cond, w, b):
    mod = jax.nn.silu(cond) @ w + b
    shift, scale, gate = jnp.split(mod[:, None, :], 3, axis=-1)
    return shift, scale, gate


def _rotate_half(x, ang):
    cos = jnp.cos(ang)[None, :, None, :].astype(x.dtype)
    sin = jnp.sin(ang)[None, :, None, :].astype(x.dtype)
    x1, x2 = jnp.split(x, 2, axis=-1)
    return jnp.concatenate([x1 * cos - x2 * sin, x1 * sin + x2 * cos], axis=-1)


def axial_rope(x):
    n_tok = x.shape[1]
    t = jnp.arange(n_tok)
    row = (t // GRID_W).astype(jnp.float32)
    col = (t % GRID_W).astype(jnp.float32)
    nf = ROPE_DIM // 4
    inv = ROPE_BASE ** (-jnp.arange(nf, dtype=jnp.float32) / nf)
    xr, xc = jnp.split(x, 2, axis=-1)
    return jnp.concatenate([_rotate_half(xr, row[:, None] * inv), _rotate_half(xc, col[:, None] * inv)], axis=-1)


def rope_tail(x):
    return jnp.concatenate([x[..., :NOPE_DIM], axial_rope(x[..., NOPE_DIM:])], axis=-1)


def block_attention(q, k, v):
    b, sq, h, d = q.shape
    nb = sq // QBLOCK
    scale = d ** -0.5
    qb = jnp.moveaxis(q.reshape(b, nb, QBLOCK, h, d), 1, 0)

    def one_block(qblk):
        s = jnp.einsum('bqhd,bkhd->bhqk', qblk, k, preferred_element_type=jnp.float32) * scale
        p = jax.nn.softmax(s, axis=-1).astype(v.dtype)
        return jnp.einsum('bhqk,bkhd->bqhd', p, v)

    out = lax.map(one_block, qb)
    return jnp.moveaxis(out, 0, 1).reshape(b, sq, h, v.shape[-1])


def mla_keys_values(ckv, kpe, w_kv_b, k_norm):
    b, s, _ = ckv.shape
    kv = (ckv @ w_kv_b).reshape(b, s, H_A, NOPE_DIM + V_DIM)
    k_nope, v = kv[..., :NOPE_DIM], kv[..., NOPE_DIM:]
    k_pe = jnp.broadcast_to(kpe[:, :, None, :], (b, s, H_A, ROPE_DIM))
    k = rms_norm(jnp.concatenate([k_nope, k_pe], axis=-1), k_norm)
    return k, v


def mlstm_chunkwise(q, k, v, i_pre, f_pre, C0, n0, m0):
    b, s, h, d = q.shape
    nc = s // CHUNK

    def chunks(a):
        return jnp.moveaxis(a.reshape((b, nc, CHUNK) + a.shape[2:]), (1, 3), (0, 2))

    xs = (chunks(q), chunks(k * (d ** -0.5)), chunks(v), chunks(i_pre), chunks(jax.nn.log_sigmoid(f_pre)))
    tri = jnp.tril(jnp.ones((CHUNK, CHUNK), dtype=bool))

    def step(carry, inp):
        C, n, m = carry
        qc, kc, vc, ic, lf = inp
        cum = jnp.cumsum(lf, axis=-1)
        inter = cum + m[..., None]
        dmat = jnp.where(tri, cum[..., :, None] - cum[..., None, :] + ic[..., None, :], -jnp.inf)
        m_t = jnp.maximum(inter, jnp.max(dmat, axis=-1))
        w_inter = jnp.exp(inter - m_t)
        sc = jnp.einsum('bhtd,bhsd->bhts', qc, kc) * jnp.exp(dmat - m_t[..., None])
        num = w_inter[..., None] * jnp.einsum('bhtd,bhde->bhte', qc, C) + jnp.einsum('bhts,bhse->bhte', sc, vc)
        qn = w_inter * jnp.einsum('bhtd,bhd->bht', qc, n) + jnp.sum(sc, axis=-1)
        h_out = num / jnp.maximum(jnp.abs(qn), jnp.exp(-m_t))[..., None]
        cum_last = cum[..., -1]
        dec = cum_last[..., None] - cum + ic
        m_new = jnp.maximum(cum_last + m, jnp.max(dec, axis=-1))
        w_state = jnp.exp(cum_last + m - m_new)
        wk = jnp.exp(dec - m_new[..., None])
        C_new = w_state[..., None, None] * C + jnp.einsum('bhs,bhsd,bhse->bhde', wk, kc, vc)
        n_new = w_state[..., None] * n + jnp.einsum('bhs,bhsd->bhd', wk, kc)
        return (C_new, n_new, m_new), h_out

    final, hs = lax.scan(step, (C0, n0, m0), xs)
    h_seq = jnp.moveaxis(hs, (0, 2), (1, 3)).reshape(b, s, h, d)
    return final, h_seq


def mlstm_bidirectional(q, k, v, i_pre, f_pre, C0, n0, m0):
    dt = q.dtype
    q, k, v, i_pre, f_pre = (a.astype(jnp.float32) for a in (q, k, v, i_pre, f_pre))
    C0, n0, m0 = (a.astype(jnp.float32) for a in (C0, n0, m0))
    (Cf, nf, mf), h_f = mlstm_chunkwise(q, k, v, i_pre[:, :, 0], f_pre[:, :, 0], C0[:, 0], n0[:, 0], m0[:, 0])

    def flip(a):
        return jnp.flip(a, axis=1)

    (Cb, nb, mb), h_b = mlstm_chunkwise(flip(q), flip(k), flip(v), flip(i_pre[:, :, 1]), flip(f_pre[:, :, 1]),
                                        C0[:, 1], n0[:, 1], m0[:, 1])
    h = (h_f + flip(h_b)).astype(dt)
    return h, jnp.stack([Cf, Cb], axis=1), jnp.stack([nf, nb], axis=1), jnp.stack([mf, mb], axis=1)


def short_conv(u, w, bias):
    up = jnp.pad(u, ((0, 0), (1, 1), (0, 0)))
    return up[:, :-2] * w[0] + up[:, 1:-1] * w[1] + up[:, 2:] * w[2] + bias


def neighbourhood_attention(q, k, v, k_ctx, v_ctx, rpb):
    b, n, h, d = q.shape
    rows = n // GRID_W
    kr = min(WIN_R, rows)
    scale = d ** -0.5
    r = jnp.arange(rows)
    band = jnp.clip(r - kr // 2, 0, rows - kr)[:, None] + jnp.arange(kr)[None, :]
    col = jnp.arange(GRID_W)
    c0 = jnp.clip(col - WIN_C // 2, 0, GRID_W - WIN_C)
    col_ok = (col[None, :] >= c0[:, None]) & (col[None, :] < c0[:, None] + WIN_C)
    dr = band - r[:, None] + (WIN_R - 1)
    dc = jnp.clip(col[None, :] - col[:, None], -(WIN_C - 1), WIN_C - 1) + (WIN_C - 1)
    bias = rpb[:, dr[:, None, :, None], dc[None, :, None, :]].astype(jnp.float32)
    qg = q.reshape(b, rows, GRID_W, h, d)
    kg = k.reshape(b, rows, GRID_W, h, d)[:, band]
    vg = v.reshape(b, rows, GRID_W, h, d)[:, band].reshape(b, rows, kr * GRID_W, h, d)
    s_win = jnp.einsum('brqhd,brkwhd->bhrqkw', qg, kg, preferred_element_type=jnp.float32) * scale + bias[None]
    s_win = jnp.where(col_ok[None, None, None, :, None, :], s_win, -jnp.inf).reshape(b, h, rows, GRID_W, kr * GRID_W)
    s_ctx = jnp.einsum('brqhd,bkhd->bhrqk', qg, k_ctx, preferred_element_type=jnp.float32) * scale
    p = jax.nn.softmax(jnp.concatenate([s_win, s_ctx], axis=-1), axis=-1).astype(v.dtype)
    p_win, p_ctx = p[..., :kr * GRID_W], p[..., kr * GRID_W:]
    out = jnp.einsum('bhrqk,brkhd->brqhd', p_win, vg) + jnp.einsum('bhrqk,bkhd->brqhd', p_ctx, v_ctx)
    return out.reshape(b, n, h, d)


def even_mixer(h, lp, ctx=None):
    b, s, _ = h.shape
    qa, kva, kpe, g_a, q_m, k_m, v_m, i_m, f_m, o_m, g_m = split_cols(h @ lp['w_in'], EVEN_SPLITS)
    q = rms_norm((rms_norm(qa, lp['q_a_norm']) @ lp['w_q_b']).reshape(b, s, H_A, QK_DIM), lp['q_norm'])
    ckv = rms_norm(kva, lp['kv_a_norm'])
    k, v = mla_keys_values(ckv, kpe, lp['w_kv_b'], lp['k_norm'])
    i_pre = i_m.reshape(b, s, 2, H_B) + lp['b_i']
    f_pre = f_m.reshape(b, s, 2, H_B) + lp['b_f']
    if ctx is None:
        C0 = jnp.zeros((b, 2, H_B, DH_B, DH_B), jnp.float32)
        n0 = jnp.zeros((b, 2, H_B, DH_B), jnp.float32)
        m0 = jnp.zeros((b, 2, H_B), jnp.float32)
    else:
        ckv_c, kpe_c, C0, n0, m0 = ctx
        q, k = rope_tail(q), rope_tail(k)
        k_c, v_c = mla_keys_values(ckv_c, kpe_c, lp['w_kv_b'], lp['k_norm'])
        k = jnp.concatenate([k_c, k], axis=1)
        v = jnp.concatenate([v_c, v], axis=1)
    att = block_attention(q, k, v).reshape(b, s, W_A) * jax.nn.silu(g_a)
    hm, C_fin, n_fin, m_fin = mlstm_bidirectional(
        q_m.reshape(b, s, H_B, DH_B), k_m.reshape(b, s, H_B, DH_B), v_m.reshape(b, s, H_B, DH_B),
        i_pre, f_pre, C0, n0, m0)
    hm = rms_norm(hm, lp['h_norm'].reshape(H_B, DH_B)).reshape(b, s, W_B)
    hm = hm * jax.nn.sigmoid(o_m) * jax.nn.silu(g_m)
    out = jnp.concatenate([att, hm], axis=-1) @ lp['w_out']
    return out, (ckv, kpe, C_fin.astype(h.dtype), n_fin.astype(h.dtype), m_fin.astype(h.dtype))


def odd_mixer(h, lp, ctx=None):
    b, s, _ = h.shape
    xc, bc, cc, g_c, q_d, k_d, v_d, g_d = split_cols(h @ lp['w_in'], ODD_SPLITS)
    conv_out = bc * short_conv(cc * xc, lp['conv_w'], lp['conv_b']) * jax.nn.silu(g_c)
    q = rms_norm(q_d.reshape(b, s, H_D, DH_D), lp['q_norm'])
    k = rms_norm(k_d.reshape(b, s, H_D, DH_D), lp['k_norm'])
    v = v_d.reshape(b, s, H_D, DH_D)
    if ctx is None:
        na = block_attention(q, k, v)
    else:
        k_c, v_c = ctx
        na = neighbourhood_attention(q, k, v, k_c, v_c, lp['rpb'])
    na = na.reshape(b, s, W_D) * jax.nn.silu(g_d)
    out = jnp.concatenate([conv_out, na], axis=-1) @ lp['w_out']
    return out, (k, v)


def setup_inputs(seed: int = 0) -> dict:
    key = jax.random.key(seed)
    keys = jax.random.split(key, 48)
    ks = iter([keys[i] for i in range(48)])

    def nrm(shape, scale=1.0):
        return scale * jax.random.normal(next(ks), shape, jnp.float32)

    def gain(shape):
        return 1.0 + nrm(shape, 0.05)

    ne = (DEPTH + 1) // 2
    no = DEPTH // 2
    D = D_MODEL
    return {
        'x_prompt': nrm((BATCH, SEQ, D)),
        'x_sample': nrm((DEC_BATCH, DEC_SEQ, D)),
        'c': nrm((DEC_BATCH, D)),
        'cache_mla_ckv': nrm((DEC_BATCH, ne, PAST_LEN, KV_RANK)),
        'cache_mla_kpe': nrm((DEC_BATCH, ne, PAST_LEN, ROPE_DIM)),
        'state_mlstm_C': nrm((DEC_BATCH, ne, 2, H_B, DH_B, DH_B), 0.05),
        'state_mlstm_n': nrm((DEC_BATCH, ne, 2, H_B, DH_B), 0.05),
        'state_mlstm_m': 2.0 + nrm((DEC_BATCH, ne, 2, H_B), 0.5),
        'cache_na_k': nrm((DEC_BATCH, no, PAST_LEN, H_D, DH_D)),
        'cache_na_v': nrm((DEC_BATCH, no, PAST_LEN, H_D, DH_D)),
        'c_ctx': nrm((D,)),
        'norm_w': gain((DEPTH, D)),
        'ada_w': nrm((DEPTH, D, 3 * D), 0.5 * D ** -0.5),
        'ada_b': nrm((DEPTH, 3 * D), 0.02),
        'ev_w_in': nrm((ne, D, EVEN_IN), D ** -0.5),
        'ev_q_a_norm': gain((ne, Q_RANK)),
        'ev_kv_a_norm': gain((ne, KV_RANK)),
        'ev_w_q_b': nrm((ne, Q_RANK, H_A * QK_DIM), Q_RANK ** -0.5),
        'ev_w_kv_b': nrm((ne, KV_RANK, H_A * (NOPE_DIM + V_DIM)), KV_RANK ** -0.5),
        'ev_q_norm': gain((ne, QK_DIM)),
        'ev_k_norm': gain((ne, QK_DIM)),
        'ev_b_i': -1.0 + nrm((ne, 2, H_B), 0.3),
        'ev_b_f': 3.0 + nrm((ne, 2, H_B), 0.5),
        'ev_h_norm': gain((ne, W_B)),
        'ev_w_out': nrm((ne, W_A + W_B, D), (W_A + W_B) ** -0.5),
        'od_w_in': nrm((no, D, ODD_IN), D ** -0.5),
        'od_conv_w': nrm((no, CONV_W, W_C), CONV_W ** -0.5),
        'od_conv_b': nrm((no, W_C), 0.02),
        'od_q_norm': gain((no, DH_D)),
        'od_k_norm': gain((no, DH_D)),
        'od_rpb': nrm((no, H_D, 2 * WIN_R - 1, 2 * WIN_C - 1), 0.5),
        'od_w_out': nrm((no, W_C + W_D, D), (W_C + W_D) ** -0.5),
    }


def reference(x_prompt, x_sample, c, cache_mla_ckv, cache_mla_kpe, state_mlstm_C, state_mlstm_n, state_mlstm_m,
              cache_na_k, cache_na_v, c_ctx, norm_w, ada_w, ada_b,
              ev_w_in, ev_q_a_norm, ev_kv_a_norm, ev_w_q_b, ev_w_kv_b, ev_q_norm, ev_k_norm, ev_b_i, ev_b_f,
              ev_h_norm, ev_w_out,
              od_w_in, od_conv_w, od_conv_b, od_q_norm, od_k_norm, od_rpb, od_w_out):
    yp, ys = x_prompt, x_sample
    l_ckv, l_kpe, l_C, l_n, l_m, l_k, l_v = [], [], [], [], [], [], []
    for l in range(DEPTH):
        j = l // 2
        sh_p, sc_p, g_p = modulation(c_ctx[None, :], ada_w[l], ada_b[l])
        sh_s, sc_s, g_s = modulation(c, ada_w[l], ada_b[l])
        hp = rms_norm(yp, norm_w[l]) * (1 + sc_p) + sh_p
        hs = rms_norm(ys, norm_w[l]) * (1 + sc_s) + sh_s
        if l % 2 == 0:
            lp = {'w_in': ev_w_in[j], 'q_a_norm': ev_q_a_norm[j], 'kv_a_norm': ev_kv_a_norm[j],
                  'w_q_b': ev_w_q_b[j], 'w_kv_b': ev_w_kv_b[j], 'q_norm': ev_q_norm[j], 'k_norm': ev_k_norm[j],
                  'b_i': ev_b_i[j], 'b_f': ev_b_f[j], 'h_norm': ev_h_norm[j], 'w_out': ev_w_out[j]}
            out_p, (ckv, kpe, C_s, n_s, m_s) = even_mixer(hp, lp)
            out_s, _ = even_mixer(hs, lp, ctx=(cache_mla_ckv[:, j], cache_mla_kpe[:, j], state_mlstm_C[:, j],
                                              state_mlstm_n[:, j], state_mlstm_m[:, j]))
            l_ckv.append(ckv)
            l_kpe.append(kpe)
            l_C.append(C_s)
            l_n.append(n_s)
            l_m.append(m_s)
        else:
            lp = {'w_in': od_w_in[j], 'conv_w': od_conv_w[j], 'conv_b': od_conv_b[j], 'q_norm': od_q_norm[j],
                  'k_norm': od_k_norm[j], 'rpb': od_rpb[j], 'w_out': od_w_out[j]}
            out_p, (k_p, v_p) = odd_mixer(hp, lp)
            out_s, _ = odd_mixer(hs, lp, ctx=(cache_na_k[:, j], cache_na_v[:, j]))
            l_k.append(k_p)
            l_v.append(v_p)
        yp = yp + g_p * out_p
        ys = ys + g_s * out_s
    new_mla_ckv = jnp.stack(l_ckv, axis=1)
    new_mla_kpe = jnp.stack(l_kpe, axis=1)
    new_mlstm_C = jnp.stack(l_C, axis=1)
    new_mlstm_n = jnp.stack(l_n, axis=1)
    new_mlstm_m = jnp.stack(l_m, axis=1)
    new_na_k = jnp.stack(l_k, axis=1)
    new_na_v = jnp.stack(l_v, axis=1)
    return (yp, ys, new_mla_ckv, new_mla_kpe, new_mlstm_C, new_mlstm_n, new_mlstm_m, new_na_k, new_na_v)
```

```python
import functools

import jax
import jax.numpy as jnp
from jax import lax
from jax.experimental import pallas as pl
from jax.experimental.pallas import tpu as pltpu

F32 = jnp.float32
BF16 = jnp.bfloat16

D_MODEL = 1024
BATCH = 16
SEQ = 256
DEPTH = 4
DEC_BATCH = 2
DEC_SEQ = 1024
PAST_LEN = 256
GRID_W = 64
EPS = 1e-6
ROPE_BASE = 10000.0
H_A = 8
NOPE_DIM = 64
ROPE_DIM = 32
V_DIM = 64
QK_DIM = NOPE_DIM + ROPE_DIM
Q_RANK = 256
KV_RANK = 128
W_A = H_A * V_DIM
H_B = 4
DH_B = 128
W_B = H_B * DH_B
CHUNK = 64
W_C = 512
H_D = 8
DH_D = 64
W_D = H_D * DH_D
WIN_R = 8
WIN_C = 16

LANES = 128
V7X_VMEM_BYTES = 64 * 1024 * 1024
VMEM_LIMIT = (V7X_VMEM_BYTES * 3) // 4
ROW_TILE = 256

EVEN_COLS = 6 * 512 + Q_RANK + KV_RANK + LANES
EV_QA_BLK = (6 * 512) // Q_RANK
EV_KVA_BLK = (6 * 512 + Q_RANK) // LANES
EV_SMALL_BLK = EV_KVA_BLK + 1
EV_SMALL_OFF = EV_SMALL_BLK * LANES
GATE_I_LANE = ROPE_DIM
GATE_F_LANE = ROPE_DIM + 2 * H_B
ODD_COLS = 4 * W_C + 4 * W_D


def _params(n_axes):
    return pltpu.CompilerParams(
        dimension_semantics=("arbitrary",) * n_axes, vmem_limit_bytes=VMEM_LIMIT)


def _dot(a, b):
    return jnp.dot(a.astype(BF16), b.astype(BF16), preferred_element_type=F32)


def _dot_nt(a, b):
    return lax.dot_general(a.astype(BF16), b.astype(BF16), (((1,), (1,)), ((), ())),
                           preferred_element_type=F32)


def _dot_tn(a, b):
    return lax.dot_general(a.astype(BF16), b.astype(BF16), (((0,), (0,)), ((), ())),
                           preferred_element_type=F32)


def _silu(x):
    return x * jax.nn.sigmoid(x)


def _softmax_parts(scores):
    m = scores[0].max(axis=-1, keepdims=True)
    for s in scores[1:]:
        m = jnp.maximum(m, s.max(axis=-1, keepdims=True))
    ps = [jnp.exp(s - m) for s in scores]
    l = ps[0].sum(axis=-1, keepdims=True)
    for p in ps[1:]:
        l = l + p.sum(axis=-1, keepdims=True)
    return ps, l


def _mod_kernel(cond_ref, w_ref, b_ref, o_ref):
    o_ref[...] = _dot(_silu(cond_ref[...]), w_ref[...]) + b_ref[...]


def _modulation(cond8, ada_w, ada_b):
    tn = D_MODEL
    return pl.pallas_call(
        _mod_kernel,
        grid=(DEPTH, 3 * D_MODEL // tn),
        in_specs=[
            pl.BlockSpec((8, D_MODEL), lambda l, j: (0, 0)),
            pl.BlockSpec((None, D_MODEL, tn), lambda l, j: (l, 0, j)),
            pl.BlockSpec((None, 1, tn), lambda l, j: (l, 0, j)),
        ],
        out_specs=pl.BlockSpec((None, 8, tn), lambda l, j: (l, 0, j)),
        out_shape=jax.ShapeDtypeStruct((DEPTH, 8, 3 * D_MODEL), F32),
        compiler_params=_params(2),
        name="modulation",
    )(cond8, ada_w, ada_b.reshape(DEPTH, 1, 3 * D_MODEL))


def _inproj_kernel(y_ref, mod_ref, nw_ref, w_ref, o_ref, *, chunk):
    x = y_ref[...]
    xn = x * lax.rsqrt(jnp.mean(x * x, axis=-1, keepdims=True) + EPS) * nw_ref[...]
    mod = mod_ref[...]
    h = (xn * (1.0 + mod[:, D_MODEL:2 * D_MODEL]) + mod[:, :D_MODEL]).astype(BF16)
    for j in range(o_ref.shape[1] // chunk):
        sl = slice(j * chunk, (j + 1) * chunk)
        o_ref[:, sl] = jnp.dot(h, w_ref[:, sl], preferred_element_type=F32)


def _in_proj(y, mod_rows, norm_w, w, name):
    t, n = y.shape[0], w.shape[1]
    tiles_per_mod = (t // mod_rows.shape[0]) // ROW_TILE
    return pl.pallas_call(
        functools.partial(_inproj_kernel, chunk=512),
        grid=(t // ROW_TILE,),
        in_specs=[
            pl.BlockSpec((ROW_TILE, D_MODEL), lambda i: (i, 0)),
            pl.BlockSpec((None, 1, 3 * D_MODEL), lambda i: (i // tiles_per_mod, 0, 0)),
            pl.BlockSpec((1, D_MODEL), lambda i: (0, 0)),
            pl.BlockSpec((D_MODEL, n), lambda i: (0, 0)),
        ],
        out_specs=pl.BlockSpec((ROW_TILE, n), lambda i: (i, 0)),
        out_shape=jax.ShapeDtypeStruct((t, n), F32),
        compiler_params=_params(1),
        name=name,
    )(y, mod_rows, norm_w.reshape(1, D_MODEL), w)


def _outproj_kernel(a_ref, b_ref, y_ref, mod_ref, w_ref, o_ref):
    ka = a_ref.shape[1]
    acc = _dot(a_ref[...], w_ref[:ka, :]) + _dot(b_ref[...], w_ref[ka:, :])
    o_ref[...] = y_ref[...] + mod_ref[:, 2 * D_MODEL:] * acc


def _out_proj(a, b, y, mod_rows, w, name):
    t = y.shape[0]
    tiles_per_mod = (t // mod_rows.shape[0]) // ROW_TILE
    return pl.pallas_call(
        _outproj_kernel,
        grid=(t // ROW_TILE,),
        in_specs=[
            pl.BlockSpec((ROW_TILE, a.shape[1]), lambda i: (i, 0)),
            pl.BlockSpec((ROW_TILE, b.shape[1]), lambda i: (i, 0)),
            pl.BlockSpec((ROW_TILE, D_MODEL), lambda i: (i, 0)),
            pl.BlockSpec((None, 1, 3 * D_MODEL), lambda i: (i // tiles_per_mod, 0, 0)),
            pl.BlockSpec(w.shape, lambda i: (0, 0)),
        ],
        out_specs=pl.BlockSpec((ROW_TILE, D_MODEL), lambda i: (i, 0)),
        out_shape=jax.ShapeDtypeStruct((t, D_MODEL), F32),
        compiler_params=_params(1),
        name=name,
    )(a, b, y, mod_rows, w)


def _rope(x, cos, sin, lane):
    partner = jnp.where((lane & 8) == 0, pltpu.roll(x, LANES - 8, 1), pltpu.roll(x, 8, 1))
    return x * cos + partner * sin


def _head_rms(x, gain):
    ss = jnp.sum(x * x, axis=-1, keepdims=True) * (1.0 / QK_DIM)
    return x * lax.rsqrt(ss + EPS) * gain


def _mla_keys(ckv, kpe_slot, wk_ref, wv_ref, gk, k_out, v_out, rope):
    cb = ckv.astype(BF16)
    kraw = jnp.dot(cb, wk_ref[...], preferred_element_type=F32)
    v_out[...] = jnp.dot(cb, wv_ref[...], preferred_element_type=F32).astype(BF16)
    for h in range(H_A):
        sl = slice(h * LANES, (h + 1) * LANES)
        ks = _head_rms(kraw[:, sl] + kpe_slot, gk)
        if rope is not None:
            ks = _rope(ks, *rope)
        k_out[:, sl] = ks.astype(BF16)


def _kpe_slot(small, lane):
    in_pe = (lane >= NOPE_DIM) & (lane < QK_DIM)
    return jnp.where(in_pe, pltpu.roll(small, NOPE_DIM, 1), 0.0)


def _mla_prep_kernel(*refs, use_rope):
    if use_rope:
        (qa_ref, kva_ref, sm_ref, qan_ref, kvn_ref, wq_ref, wk_ref, wv_ref, gq_ref, gk_ref,
         cos_ref, sin_ref, q_out, k_out, v_out, ckv_out) = refs
    else:
        (qa_ref, kva_ref, sm_ref, qan_ref, kvn_ref, wq_ref, wk_ref, wv_ref, gq_ref, gk_ref,
         q_out, k_out, v_out, ckv_out) = refs
    tm = qa_ref.shape[0]
    lane = lax.broadcasted_iota(jnp.int32, (tm, LANES), 1)
    rope = (cos_ref[...], sin_ref[...], lane) if use_rope else None

    qa = qa_ref[...]
    qan = qa * lax.rsqrt(jnp.mean(qa * qa, axis=-1, keepdims=True) + EPS) * qan_ref[...]
    qraw = _dot(qan, wq_ref[...])
    gq = gq_ref[...]
    for h in range(H_A):
        sl = slice(h * LANES, (h + 1) * LANES)
        qs = _head_rms(qraw[:, sl], gq)
        if use_rope:
            qs = _rope(qs, *rope)
        q_out[:, sl] = (qs * (QK_DIM ** -0.5)).astype(BF16)

    kva = kva_ref[...]
    ckv = kva * lax.rsqrt(jnp.mean(kva * kva, axis=-1, keepdims=True) + EPS) * kvn_ref[...]
    ckv_out[...] = ckv
    _mla_keys(ckv, _kpe_slot(sm_ref[...], lane), wk_ref, wv_ref, gk_ref[...], k_out, v_out, rope)


def _mla_prep(u, wts, rope_tabs, name):
    t = u.shape[0]
    use_rope = rope_tabs is not None
    const = lambda i: (0, 0)
    in_specs = [
        pl.BlockSpec((ROW_TILE, Q_RANK), lambda i: (i, EV_QA_BLK)),
        pl.BlockSpec((ROW_TILE, KV_RANK), lambda i: (i, EV_KVA_BLK)),
        pl.BlockSpec((ROW_TILE, LANES), lambda i: (i, EV_SMALL_BLK)),
        pl.BlockSpec((1, Q_RANK), const),
        pl.BlockSpec((1, KV_RANK), const),
        pl.BlockSpec((Q_RANK, H_A * LANES), const),
        pl.BlockSpec((KV_RANK, H_A * LANES), const),
        pl.BlockSpec((KV_RANK, W_A), const),
        pl.BlockSpec((1, LANES), const),
        pl.BlockSpec((1, LANES), const),
    ]
    args = [u, u, u, wts["qan"], wts["kvn"], wts["wq"], wts["wk"], wts["wv"], wts["gq"], wts["gk"]]
    if use_rope:
        tiles_per_seq = DEC_SEQ // ROW_TILE
        in_specs += [pl.BlockSpec((ROW_TILE, LANES), lambda i: (i % tiles_per_seq, 0))] * 2
        args += list(rope_tabs)
    return pl.pallas_call(
        functools.partial(_mla_prep_kernel, use_rope=use_rope),
        grid=(t // ROW_TILE,),
        in_specs=in_specs,
        out_specs=[
            pl.BlockSpec((ROW_TILE, H_A * LANES), lambda i: (i, 0)),
            pl.BlockSpec((ROW_TILE, H_A * LANES), lambda i: (i, 0)),
            pl.BlockSpec((ROW_TILE, W_A), lambda i: (i, 0)),
            pl.BlockSpec((ROW_TILE, KV_RANK), lambda i: (i, 0)),
        ],
        out_shape=[
            jax.ShapeDtypeStruct((t, H_A * LANES), BF16),
            jax.ShapeDtypeStruct((t, H_A * LANES), BF16),
            jax.ShapeDtypeStruct((t, W_A), BF16),
            jax.ShapeDtypeStruct((t, KV_RANK), F32),
        ],
        compiler_params=_params(1),
        name=name,
    )(*args)


def _mla_ctx_kernel(ckv_ref, kpe_ref, wk_ref, wv_ref, gk_ref, k_out, v_out):
    lane = lax.broadcasted_iota(jnp.int32, (ckv_ref.shape[0], LANES), 1)
    _mla_keys(ckv_ref[...], _kpe_slot(kpe_ref[...], lane), wk_ref, wv_ref, gk_ref[...],
              k_out, v_out, None)


def _mla_ctx(ckv_c, kpe_c, wts, name):
    t = ckv_c.shape[0]
    const = lambda i: (0, 0)
    return pl.pallas_call(
        _mla_ctx_kernel,
        grid=(t // ROW_TILE,),
        in_specs=[
            pl.BlockSpec((ROW_TILE, KV_RANK), lambda i: (i, 0)),
            pl.BlockSpec((ROW_TILE, LANES), lambda i: (i, 0)),
            pl.BlockSpec((KV_RANK, H_A * LANES), const),
            pl.BlockSpec((KV_RANK, W_A), const),
            pl.BlockSpec((1, LANES), const),
        ],
        out_specs=[
            pl.BlockSpec((ROW_TILE, H_A * LANES), lambda i: (i, 0)),
            pl.BlockSpec((ROW_TILE, W_A), lambda i: (i, 0)),
        ],
        out_shape=[
            jax.ShapeDtypeStruct((t, H_A * LANES), BF16),
            jax.ShapeDtypeStruct((t, W_A), BF16),
        ],
        compiler_params=_params(1),
        name=name,
    )(ckv_c, kpe_c, wts["wk"], wts["wv"], wts["gk"])


def _mla_attn_kernel(*refs, has_ctx):
    if has_ctx:
        q_ref, k_ref, v_ref, kc_ref, vc_ref, g_ref, o_ref = refs
    else:
        q_ref, k_ref, v_ref, g_ref, o_ref = refs
    tq = q_ref.shape[0]
    lane = lax.broadcasted_iota(jnp.int32, (tq, LANES), 1)
    for j in range(H_A // 2):
        pair = slice(j * LANES, (j + 1) * LANES)
        outs = []
        for h in (2 * j, 2 * j + 1):
            sl = slice(h * LANES, (h + 1) * LANES)
            q = q_ref[:, sl]
            scores = [_dot_nt(q, k_ref[:, sl])]
            if has_ctx:
                scores = [_dot_nt(q, kc_ref[:, sl])] + scores
            ps, l = _softmax_parts(scores)
            o = _dot(ps[-1], v_ref[:, pair])
            if has_ctx:
                o = o + _dot(ps[0], vc_ref[:, pair])
            outs.append(o / l)
        o_ref[:, pair] = jnp.where(lane < V_DIM, outs[0], outs[1]) * _silu(g_ref[:, pair])


def _mla_attn(q, k, v, ctx, u, n_batch, seq, q_tile, name):
    has_ctx = ctx is not None
    nq = seq // q_tile
    in_specs = [
        pl.BlockSpec((q_tile, H_A * LANES), lambda b, i: (b * nq + i, 0)),
        pl.BlockSpec((seq, H_A * LANES), lambda b, i: (b, 0)),
        pl.BlockSpec((seq, W_A), lambda b, i: (b, 0)),
    ]
    args = [q, k, v]
    if has_ctx:
        in_specs += [
            pl.BlockSpec((PAST_LEN, H_A * LANES), lambda b, i: (b, 0)),
            pl.BlockSpec((PAST_LEN, W_A), lambda b, i: (b, 0)),
        ]
        args += list(ctx)
    in_specs.append(pl.BlockSpec((q_tile, W_A), lambda b, i: (b * nq + i, 0)))
    args.append(u)
    return pl.pallas_call(
        functools.partial(_mla_attn_kernel, has_ctx=has_ctx),
        grid=(n_batch, nq),
        in_specs=in_specs,
        out_specs=pl.BlockSpec((q_tile, W_A), lambda b, i: (b * nq + i, 0)),
        out_shape=jax.ShapeDtypeStruct((n_batch * seq, W_A), F32),
        compiler_params=_params(2),
        name=name,
    )(*args)


def _mlstm_kernel(*refs, seq, has_state):
    if has_state:
        (q_ref, k_ref, v_ref, o_ref, g_ref, sm_ref, bias_ref, hn_ref, c0_ref, n0_ref, m0_ref,
         out_ref, caug, mrow, xcol, xrow, hbuf) = refs
    else:
        (q_ref, k_ref, v_ref, o_ref, g_ref, sm_ref, bias_ref, hn_ref,
         out_ref, cfin_ref, nfin_ref, mfin_ref, caug, mrow, xcol, xrow, hbuf) = refs
    nc = seq // CHUNK
    n_prob = 2 * H_B
    lane = lax.broadcasted_iota(jnp.int32, (CHUNK, LANES), 1)
    tpos = lax.broadcasted_iota(jnp.int32, (CHUNK, CHUNK), 0)
    spos = lax.broadcasted_iota(jnp.int32, (CHUNK, CHUNK), 1)
    tri_lo = (spos <= tpos).astype(F32)
    tri_hi = (spos >= tpos).astype(F32)
    fwd_f = (lane >= GATE_F_LANE) & (lane < GATE_F_LANE + H_B)
    bwd_f = (lane >= GATE_F_LANE + H_B) & (lane < GATE_F_LANE + 2 * H_B)
    one_col = (lane == 0).astype(F32)

    for c in range(nc):
        rows = slice(c * CHUNK, (c + 1) * CHUNK)
        g = sm_ref[rows, :] + bias_ref[...]
        lf = jax.nn.log_sigmoid(g)
        cum_f = jnp.dot(tri_lo, lf, precision=lax.Precision.HIGHEST, preferred_element_type=F32)
        cum_b = jnp.dot(tri_hi, lf, precision=lax.Precision.HIGHEST, preferred_element_type=F32)
        x = jnp.where(fwd_f, cum_f, jnp.where(bwd_f, cum_b, g))
        xcol[rows, :] = x
        xt = jnp.concatenate([x, jnp.zeros_like(x)], axis=0).T
        xrow[c] = xt[GATE_I_LANE:GATE_I_LANE + 4 * H_B, :]

    if has_state:
        for idx in range(n_prob):
            d, h = divmod(idx, H_B)
            ncol = n0_ref[d, h:h + 1, :]
            nt = jnp.concatenate([ncol, jnp.zeros((LANES - 1, DH_B), F32)], axis=0).T
            caug[idx, :, :DH_B] = c0_ref[d, h]
            caug[idx, :, DH_B:] = nt
        mrow[...] = m0_ref[...]
    else:
        caug[...] = jnp.zeros_like(caug)
        mrow[...] = jnp.zeros_like(mrow)

    def step(s, carry):
        for d in range(2):
            c = s if d == 0 else nc - 1 - s
            r0 = pl.multiple_of(c * CHUNK, CHUNK)
            xc = xcol[pl.ds(r0, CHUNK), :]
            xr = xrow[c]
            mask = (spos <= tpos) if d == 0 else (spos >= tpos)
            for h in range(H_B):
                idx = d * H_B + h
                hs = slice(h * DH_B, (h + 1) * DH_B)
                qc = q_ref[pl.ds(r0, CHUNK), hs].astype(BF16)
                kc = k_ref[pl.ds(r0, CHUNK), hs] * (DH_B ** -0.5)
                vaug = jnp.concatenate([v_ref[pl.ds(r0, CHUNK), hs], one_col], axis=1).astype(BF16)
                ic_col = xc[:, GATE_I_LANE + idx:GATE_I_LANE + idx + 1]
                cum_col = xc[:, GATE_F_LANE + idx:GATE_F_LANE + idx + 1]
                ic_row = xr[idx:idx + 1, :CHUNK]
                cum_row = xr[n_prob + idx:n_prob + idx + 1, :CHUNK]
                m_prev = mrow[idx:idx + 1, 0:1]

                inter = cum_col + m_prev
                dmat = jnp.where(mask, cum_col - cum_row + ic_row, -jnp.inf)
                m_t = jnp.maximum(inter, dmat.max(axis=-1, keepdims=True))
                w_inter = jnp.exp(inter - m_t)
                sc = _dot_nt(qc, kc) * jnp.exp(dmat - m_t)
                c_old = caug[idx]
                num = w_inter * _dot(qc, c_old) + _dot(sc, vaug)
                qn = num[:, DH_B:DH_B + 1]
                hbuf[d, pl.ds(r0, CHUNK), hs] = num[:, :DH_B] / jnp.maximum(jnp.abs(qn), jnp.exp(-m_t))

                cum_last = cum_col[CHUNK - 1:CHUNK, :] if d == 0 else cum_col[0:1, :]
                dec = cum_last - cum_col + ic_col
                m_new = jnp.maximum(cum_last + m_prev, dec.max(axis=0, keepdims=True))
                kw = kc * jnp.exp(dec - m_new)
                caug[idx] = jnp.exp(cum_last + m_prev - m_new) * c_old + _dot_tn(kw, vaug)
                mrow[idx:idx + 1, :] = jnp.broadcast_to(m_new, (1, LANES))
        return carry

    lax.fori_loop(0, nc, step, 0)

    hsum = hbuf[0] + hbuf[1]
    for h in range(H_B):
        hs = slice(h * DH_B, (h + 1) * DH_B)
        x = hsum[:, hs]
        xn = x * lax.rsqrt(jnp.mean(x * x, axis=-1, keepdims=True) + EPS) * hn_ref[:, hs]
        out_ref[:, hs] = xn * jax.nn.sigmoid(o_ref[:, hs]) * _silu(g_ref[:, hs])

    if not has_state:
        for idx in range(n_prob):
            d, h = divmod(idx, H_B)
            cfin_ref[d, h] = caug[idx, :, :DH_B]
            nfin_ref[idx:idx + 1, :] = caug[idx, :, DH_B:].T[0:1, :]
        mfin_ref[...] = mrow[...]


def _mlstm(u, gate_bias, h_norm, state, n_batch, seq, name):
    has_state = state is not None
    col = lambda blk: pl.BlockSpec((seq, W_B), lambda b: (b, blk))
    in_specs = [col(1), col(2), col(3), col(4), col(5),
                pl.BlockSpec((seq, LANES), lambda b: (b, EV_SMALL_BLK)),
                pl.BlockSpec((1, LANES), lambda b: (0, 0)),
                pl.BlockSpec((1, W_B), lambda b: (0, 0))]
    args = [u, u, u, u, u, u, gate_bias, h_norm]
    out_specs = [pl.BlockSpec((seq, W_B), lambda b: (b, 0))]
    out_shape = [jax.ShapeDtypeStruct((n_batch * seq, W_B), F32)]
    if has_state:
        c0, n0, m0 = state
        in_specs += [
            pl.BlockSpec((None, 2, H_B, DH_B, DH_B), lambda b: (b, 0, 0, 0, 0)),
            pl.BlockSpec((None, 2, H_B, DH_B), lambda b: (b, 0, 0, 0)),
            pl.BlockSpec((None, 2 * H_B, LANES), lambda b: (b, 0, 0)),
        ]
        args += [c0, n0, m0]
    else:
        out_specs += [
            pl.BlockSpec((None, 2, H_B, DH_B, DH_B), lambda b: (b, 0, 0, 0, 0)),
            pl.BlockSpec((None, 2 * H_B, DH_B), lambda b: (b, 0, 0)),
            pl.BlockSpec((None, 2 * H_B, LANES), lambda b: (b, 0, 0)),
        ]
        out_shape += [
            jax.ShapeDtypeStruct((n_batch, 2, H_B, DH_B, DH_B), F32),
            jax.ShapeDtypeStruct((n_batch, 2 * H_B, DH_B), F32),
            jax.ShapeDtypeStruct((n_batch, 2 * H_B, LANES), F32),
        ]
    return pl.pallas_call(
        functools.partial(_mlstm_kernel, seq=seq, has_state=has_state),
        grid=(n_batch,),
        in_specs=in_specs,
        out_specs=out_specs,
        out_shape=out_shape,
        scratch_shapes=[
            pltpu.VMEM((2 * H_B, DH_B, 2 * DH_B), F32),
            pltpu.VMEM((2 * H_B, LANES), F32),
            pltpu.VMEM((seq, LANES), F32),
            pltpu.VMEM((seq // CHUNK, 4 * H_B, LANES), F32),
            pltpu.VMEM((2, seq, W_B), F32),
        ],
        compiler_params=_params(1),
        name=name,
    )(*args)


def _short_conv(xc, bc, cc, gc, w, b):
    s = xc.shape[0]
    z = cc * xc
    row = lax.broadcasted_iota(jnp.int32, z.shape, 0)
    zp = jnp.where(row == 0, 0.0, pltpu.roll(z, 1, 0))
    zn = jnp.where(row == s - 1, 0.0, pltpu.roll(z, s - 1, 0))
    conv = zp * w[0:1, :] + z * w[1:2, :] + zn * w[2:3, :] + b
    return bc * conv * _silu(gc)


def _pair_rms(x, gain, low):
    x2 = x * x
    sa = jnp.sum(jnp.where(low, x2, 0.0), axis=-1, keepdims=True) * (1.0 / DH_D)
    sb = jnp.sum(jnp.where(low, 0.0, x2), axis=-1, keepdims=True) * (1.0 / DH_D)
    return x * jnp.where(low, lax.rsqrt(sa + EPS), lax.rsqrt(sb + EPS)) * gain


def _conv_kernel(xc_ref, bc_ref, cc_ref, gc_ref, cw_ref, cb_ref, o_ref):
    o_ref[...] = _short_conv(xc_ref[...], bc_ref[...], cc_ref[...], gc_ref[...],
                             cw_ref[...], cb_ref[...])


def _conv(u, wts, n_batch, seq, name):
    col = lambda blk: pl.BlockSpec((seq, W_C), lambda b: (b, blk))
    const = lambda b: (0, 0)
    return pl.pallas_call(
        _conv_kernel,
        grid=(n_batch,),
        in_specs=[col(0), col(1), col(2), col(3),
                  pl.BlockSpec((3, W_C), const), pl.BlockSpec((1, W_C), const)],
        out_specs=pl.BlockSpec((seq, W_C), lambda b: (b, 0)),
        out_shape=jax.ShapeDtypeStruct((n_batch * seq, W_C), F32),
        compiler_params=_params(1),
        name=name,
    )(u, u, u, u, wts["conv_w"], wts["conv_b"])


def _odd_prompt_kernel(q_ref, k_ref, v_ref, g_ref, qg_ref, kg_ref, na_out, k_out, v_out):
    s = q_ref.shape[0]
    v_out[...] = v_ref[...]
    low = lax.broadcasted_iota(jnp.int32, (s, LANES), 1) < DH_D
    for j in range(H_D // 2):
        pair = slice(j * LANES, (j + 1) * LANES)
        qn = _pair_rms(q_ref[:, pair], qg_ref[...], low) * (DH_D ** -0.5)
        kn = _pair_rms(k_ref[:, pair], kg_ref[...], low)
        k_out[:, pair] = kn
        kb = kn.astype(BF16)
        vb = v_ref[:, pair].astype(BF16)
        outs = []
        for sel in (low, jnp.logical_not(low)):
            ps, l = _softmax_parts([_dot_nt(jnp.where(sel, qn, 0.0), kb)])
            outs.append(_dot(ps[0], vb) / l)
        na_out[:, pair] = jnp.where(low, outs[0], outs[1]) * _silu(g_ref[:, pair])


def _odd_prompt(u, wts, name):
    col = lambda blk: pl.BlockSpec((SEQ, W_C), lambda b: (b, blk))
    const = lambda b: (0, 0)
    t = u.shape[0]
    return pl.pallas_call(
        _odd_prompt_kernel,
        grid=(BATCH,),
        in_specs=[col(i) for i in range(4, 8)] + [
            pl.BlockSpec((1, LANES), const), pl.BlockSpec((1, LANES), const)],
        out_specs=[pl.BlockSpec((SEQ, W_D), lambda b: (b, 0))] * 3,
        out_shape=[jax.ShapeDtypeStruct((t, W_D), F32)] * 3,
        compiler_params=_params(1),
        name=name,
    )(u, u, u, u, wts["qg"], wts["kg"])


def _odd_sample_kernel(q_ref, k_ref, v_ref, g_ref, kctx_ref, vctx_ref, qg_ref, kg_ref, rpb_ref,
                       na_out, bias, q_s, k_s, v_s, kc_s, vc_s):
    rows = DEC_SEQ // GRID_W
    n_dr = 2 * WIN_R - 1

    @pl.when(pl.program_id(0) == 0)
    def _():
        lane = lax.broadcasted_iota(jnp.int32, (GRID_W, LANES), 1)
        qcol = lax.broadcasted_iota(jnp.int32, (GRID_W, LANES), 0)
        kcol = lane & (GRID_W - 1)
        c0 = jnp.clip(qcol - WIN_C // 2, 0, GRID_W - WIN_C)
        ok = (kcol >= c0) & (kcol < c0 + WIN_C)
        for h in range(H_D):
            for off in range(WIN_R):
                for p in range(WIN_R // 2):
                    dr0 = (WIN_R - 1) - off + 2 * p
                    r_a = jnp.broadcast_to(rpb_ref[h * n_dr + dr0:h * n_dr + dr0 + 1, :], (GRID_W, LANES))
                    r_b = jnp.broadcast_to(rpb_ref[h * n_dr + dr0 + 1:h * n_dr + dr0 + 2, :], (GRID_W, LANES))
                    t_a = pltpu.roll(r_a, LANES - (WIN_C - 1), 1, stride=1, stride_axis=0)
                    t_b = pltpu.roll(r_b, GRID_W - (WIN_C - 1), 1, stride=1, stride_axis=0)
                    tile = jnp.where(lane < GRID_W, t_a, t_b)
                    bias[h, off, :, p * LANES:(p + 1) * LANES] = jnp.where(ok, tile, -jnp.inf)

    low_s = lax.broadcasted_iota(jnp.int32, (DEC_SEQ, LANES), 1) < DH_D
    for j in range(H_D // 2):
        pair = slice(j * LANES, (j + 1) * LANES)
        q_s[:, pair] = _pair_rms(q_ref[:, pair], qg_ref[...], low_s) * (DH_D ** -0.5)
        k_s[:, pair] = _pair_rms(k_ref[:, pair], kg_ref[...], low_s).astype(BF16)
    v_s[...] = v_ref[...].astype(BF16)
    kc_s[...] = kctx_ref[...].astype(BF16)
    vc_s[...] = vctx_ref[...].astype(BF16)
    low =lax.broadcasted_iota(jnp.int32, (GRID_W, LANES), 1) < DH_D

    def row_step(r, carry):
        start = jnp.clip(r - WIN_R // 2, 0, rows - WIN_R)
        off = r - start
        q0 = pl.multiple_of(r * GRID_W, GRID_W)
        k0 = pl.multiple_of(start * GRID_W, GRID_W)
        for j in range(H_D // 2):
            pair = slice(j * LANES, (j + 1) * LANES)
            qn = q_s[pl.ds(q0, GRID_W), pair]
            kwin = k_s[pl.ds(k0, WIN_R * GRID_W), pair]
            vwin = v_s[pl.ds(k0, WIN_R * GRID_W), pair]
            outs = []
            for hh, sel in enumerate((low, jnp.logical_not(low))):
                qm = jnp.where(sel, qn, 0.0)
                s_win = _dot_nt(qm, kwin) + bias[2 * j + hh, off]
                s_ctx = _dot_nt(qm, kc_s[:, pair])
                ps, l = _softmax_parts([s_win, s_ctx])
                outs.append((_dot(ps[0], vwin) + _dot(ps[1], vc_s[:, pair])) / l)
            na_out[pl.ds(q0, GRID_W), pair] = (
                jnp.where(low, outs[0], outs[1]) * _silu(g_ref[pl.ds(q0, GRID_W), pair]))
        return carry

    lax.fori_loop(0, rows, row_step, 0)


def _odd_sample(u, k_ctx, v_ctx, j, wts, name):
    col = lambda blk: pl.BlockSpec((DEC_SEQ, W_C), lambda b: (b, blk))
    const = lambda b: (0, 0)
    ctx = pl.BlockSpec((None, None, PAST_LEN, W_D), lambda b: (b, j, 0, 0))
    t = u.shape[0]
    return pl.pallas_call(
        _odd_sample_kernel,
        grid=(DEC_BATCH,),
        in_specs=[col(i) for i in range(4, 8)] + [ctx, ctx] + [
            pl.BlockSpec((1, LANES), const), pl.BlockSpec((1, LANES), const),
            pl.BlockSpec(wts["rpb"].shape, const)],
        out_specs=pl.BlockSpec((DEC_SEQ, W_D), lambda b: (b, 0)),
        out_shape=jax.ShapeDtypeStruct((t, W_D), F32),
        scratch_shapes=[
            pltpu.VMEM((H_D, WIN_R, GRID_W, WIN_R * GRID_W), F32),
            pltpu.VMEM((DEC_SEQ, W_D), F32),
            pltpu.VMEM((DEC_SEQ, W_D), BF16),
            pltpu.VMEM((DEC_SEQ, W_D), BF16),
            pltpu.VMEM((PAST_LEN, W_D), BF16),
            pltpu.VMEM((PAST_LEN, W_D), BF16),
        ],
        compiler_params=_params(1),
        name=name,
    )(u, u, u, u, k_ctx, v_ctx, wts["qg"], wts["kg"], wts["rpb"])


def _even_weights(w_in, q_a_norm, kv_a_norm, w_q_b, w_kv_b, q_norm, k_norm, b_i, b_f, h_norm, w_out):
    qa, kva, kpe, g_a, q_m, k_m, v_m, i_m, f_m, o_m, g_m = jnp.split(
        w_in, [256, 384, 416, 928, 1440, 1952, 2464, 2472, 2480, 2992], axis=1)
    small = jnp.concatenate([kpe, i_m, f_m, jnp.zeros((D_MODEL, LANES - ROPE_DIM - 4 * H_B), F32)], axis=1)
    w_perm = jnp.concatenate([g_a, q_m, k_m, v_m, o_m, g_m, qa, kva, small], axis=1).astype(BF16)
    pad_head = lambda g: jnp.pad(g, (0, LANES - QK_DIM)).reshape(1, LANES)
    wq = jnp.pad(w_q_b.reshape(Q_RANK, H_A, QK_DIM), ((0, 0), (0, 0), (0, LANES - QK_DIM)))
    wkv = w_kv_b.reshape(KV_RANK, H_A, NOPE_DIM + V_DIM)
    wk = jnp.pad(wkv[:, :, :NOPE_DIM], ((0, 0), (0, 0), (0, LANES - NOPE_DIM)))
    gate_bias = jnp.concatenate([jnp.zeros((ROPE_DIM,), F32), b_i.reshape(-1), b_f.reshape(-1),
                                 jnp.zeros((LANES - ROPE_DIM - 4 * H_B,), F32)]).reshape(1, LANES)
    return dict(
        w_in=w_perm,
        qan=q_a_norm.reshape(1, Q_RANK), kvn=kv_a_norm.reshape(1, KV_RANK),
        wq=wq.reshape(Q_RANK, H_A * LANES).astype(BF16),
        wk=wk.reshape(KV_RANK, H_A * LANES).astype(BF16),
        wv=wkv[:, :, NOPE_DIM:].reshape(KV_RANK, W_A).astype(BF16),
        gq=pad_head(q_norm), gk=pad_head(k_norm),
        gate_bias=gate_bias, h_norm=h_norm.reshape(1, W_B), w_out=w_out.astype(BF16))


def _odd_weights(w_in, conv_w, conv_b, q_norm, k_norm, rpb, w_out):
    n_dr, n_dc = 2 * WIN_R - 1, 2 * WIN_C - 1
    rpb_rows = jnp.pad(rpb.reshape(H_D * n_dr, n_dc), ((0, LANES - H_D * n_dr), (0, LANES - n_dc)))
    return dict(
        w_in=w_in.astype(BF16), conv_w=conv_w, conv_b=conv_b.reshape(1, W_C),
        qg=jnp.tile(q_norm, 2).reshape(1, LANES), kg=jnp.tile(k_norm, 2).reshape(1, LANES),
        rpb=rpb_rows, w_out=w_out.astype(BF16))


def _rope_tables():
    t = jnp.arange(DEC_SEQ)
    nf = ROPE_DIM // 4
    inv = ROPE_BASE ** (-jnp.arange(nf, dtype=F32) / nf)
    ang_r = (t // GRID_W).astype(F32)[:, None] * inv
    ang_c = (t % GRID_W).astype(F32)[:, None] * inv
    one = jnp.ones((DEC_SEQ, NOPE_DIM), F32)
    zero = jnp.zeros((DEC_SEQ, NOPE_DIM), F32)
    cr, sr, cc, sc = jnp.cos(ang_r), jnp.sin(ang_r), jnp.cos(ang_c), jnp.sin(ang_c)
    cos = jnp.concatenate([one, cr, cr, cc, cc, one[:, :LANES - QK_DIM]], axis=1)
    sin = jnp.concatenate([zero, -sr, sr, -sc, sc, zero[:, :LANES - QK_DIM]], axis=1)
    return cos, sin


def kernel(x_prompt, x_sample, c, cache_mla_ckv, cache_mla_kpe, state_mlstm_C, state_mlstm_n, state_mlstm_m, cache_na_k, cache_na_v, c_ctx, norm_w, ada_w, ada_b, ev_w_in, ev_q_a_norm, ev_kv_a_norm, ev_w_q_b, ev_w_kv_b, ev_q_norm, ev_k_norm, ev_b_i, ev_b_f, ev_h_norm, ev_w_out, od_w_in, od_conv_w, od_conv_b, od_q_norm, od_k_norm, od_rpb, od_w_out):
    yp = x_prompt.reshape(BATCH * SEQ, D_MODEL)
    ys = x_sample.reshape(DEC_BATCH * DEC_SEQ, D_MODEL)
    cond8 = jnp.concatenate([c_ctx[None, :], c, jnp.zeros((8 - 1 - DEC_BATCH, D_MODEL), F32)], axis=0)
    mod = _modulation(cond8, ada_w, ada_b)
    rope_tabs = _rope_tables()
    na_k_ctx = cache_na_k.reshape(DEC_BATCH, DEPTH // 2, PAST_LEN, W_D)
    na_v_ctx = cache_na_v.reshape(DEC_BATCH, DEPTH // 2, PAST_LEN, W_D)

    l_ckv, l_kpe, l_c, l_n, l_m, l_k, l_v = [], [], [], [], [], [], []
    for l in range(DEPTH):
        j = l // 2
        mod_p = mod[l, 0:1].reshape(1, 1, 3 * D_MODEL)
        mod_s = mod[l, 1:1 + DEC_BATCH].reshape(DEC_BATCH, 1, 3 * D_MODEL)
        if l % 2 == 0:
            w = _even_weights(ev_w_in[j], ev_q_a_norm[j], ev_kv_a_norm[j], ev_w_q_b[j], ev_w_kv_b[j],
                              ev_q_norm[j], ev_k_norm[j], ev_b_i[j], ev_b_f[j], ev_h_norm[j], ev_w_out[j])
            up = _in_proj(yp, mod_p, norm_w[l], w["w_in"], f"in_proj_p{l}")
            us = _in_proj(ys, mod_s, norm_w[l], w["w_in"], f"in_proj_s{l}")
            qp, kp, vp, ckv_p = _mla_prep(up, w, None, f"mla_prep_p{l}")
            att_p = _mla_attn(qp, kp, vp, None, up, BATCH, SEQ, SEQ, f"mla_attn_p{l}")
            hm_p, c_fin, n_fin, m_fin = _mlstm(up, w["gate_bias"], w["h_norm"], None, BATCH, SEQ,
                                               f"mlstm_p{l}")
            qs, ks, vs, _ = _mla_prep(us, w, rope_tabs, f"mla_prep_s{l}")
            kpe_c = jnp.pad(cache_mla_kpe[:, j].reshape(DEC_BATCH * PAST_LEN, ROPE_DIM),
                            ((0, 0), (0, LANES - ROPE_DIM)))
            kc, vc = _mla_ctx(cache_mla_ckv[:, j].reshape(DEC_BATCH * PAST_LEN, KV_RANK), kpe_c, w,
                              f"mla_ctx{l}")
            att_s = _mla_attn(qs, ks, vs, (kc, vc), us, DEC_BATCH, DEC_SEQ, ROW_TILE, f"mla_attn_s{l}")
            m0 = jnp.broadcast_to(state_mlstm_m[:, j].reshape(DEC_BATCH, 2 * H_B, 1),
                                  (DEC_BATCH, 2 * H_B, LANES))
            (hm_s,) = _mlstm(us, w["gate_bias"], w["h_norm"],
                             (state_mlstm_C[:, j], state_mlstm_n[:, j], m0), DEC_BATCH, DEC_SEQ,
                             f"mlstm_s{l}")
            yp = _out_proj(att_p, hm_p, yp, mod_p, w["w_out"], f"out_proj_p{l}")
            ys = _out_proj(att_s, hm_s, ys, mod_s, w["w_out"], f"out_proj_s{l}")
            l_ckv.append(ckv_p.reshape(BATCH, SEQ, KV_RANK))
            l_kpe.append(up[:, EV_SMALL_OFF:EV_SMALL_OFF + ROPE_DIM].reshape(BATCH, SEQ, ROPE_DIM))
            l_c.append(c_fin)
            l_n.append(n_fin.reshape(BATCH, 2, H_B, DH_B))
            l_m.append(m_fin[:, :, 0].reshape(BATCH, 2, H_B))
        else:
            w = _odd_weights(od_w_in[j], od_conv_w[j], od_conv_b[j], od_q_norm[j], od_k_norm[j],
                             od_rpb[j], od_w_out[j])
            up = _in_proj(yp, mod_p, norm_w[l], w["w_in"], f"in_proj_p{l}")
            us = _in_proj(ys, mod_s, norm_w[l], w["w_in"], f"in_proj_s{l}")
            conv_p = _conv(up, w, BATCH, SEQ, f"conv_p{l}")
            conv_s = _conv(us, w, DEC_BATCH, DEC_SEQ, f"conv_s{l}")
            na_p, k_p, v_p = _odd_prompt(up, w, f"odd_prompt{l}")
            na_s = _odd_sample(us, na_k_ctx, na_v_ctx, j, w, f"odd_sample{l}")
            yp = _out_proj(conv_p, na_p, yp, mod_p, w["w_out"], f"out_proj_p{l}")
            ys = _out_proj(conv_s, na_s, ys, mod_s, w["w_out"], f"out_proj_s{l}")
            l_k.append(k_p.reshape(BATCH, SEQ, H_D, DH_D))
            l_v.append(v_p.reshape(BATCH, SEQ, H_D, DH_D))

    return (yp.reshape(BATCH, SEQ, D_MODEL), ys.reshape(DEC_BATCH, DEC_SEQ, D_MODEL),
            jnp.stack(l_ckv, axis=1), jnp.stack(l_kpe, axis=1), jnp.stack(l_c, axis=1),
            jnp.stack(l_n, axis=1), jnp.stack(l_m, axis=1), jnp.stack(l_k, axis=1),
            jnp.stack(l_v, axis=1))
```
